```python
import jax, jax.numpy as jnp
from jax import lax
import numpy as np

D_MODEL = 1024
BATCH = 32
SEQ = 2048
DEPTH = 1

HEAD_DIM = 64
N_Q_HEADS = 8
N_KV_HEADS = 2
WINDOW = 128
ATT_BLOCK = 128
N_BUCKETS = 32
MAX_DISTANCE = 128
REC_HEADS = 4
REC_KEY_DIM = 128
REC_VAL_DIM = 128
REC_CHUNK = 64
D_FF = 2816
PLE_DIM = 256
EPS = 1e-6

ATT_Q_W = N_Q_HEADS * HEAD_DIM
ATT_KV_W = N_KV_HEADS * HEAD_DIM
REC_K_W = REC_HEADS * REC_KEY_DIM
REC_V_W = REC_HEADS * REC_VAL_DIM
IN_W = ATT_Q_W + 2 * ATT_KV_W + 2 * REC_K_W + 2 * REC_V_W + 2 * D_MODEL

kernel_name = "hybrid_swa_hgrn2_macaron_block"


def _split_points():
    widths = [ATT_Q_W, ATT_KV_W, ATT_KV_W, REC_K_W, REC_K_W, REC_V_W, REC_V_W, D_MODEL]
    return [int(v) for v in np.cumsum(widths)]


def rms_norm(x, g):
    xf = x.astype(jnp.float32)
    y = xf * lax.rsqrt(jnp.mean(xf * xf, axis=-1, keepdims=True) + EPS)
    return (y * g.astype(jnp.float32)).astype(x.dtype)


def swiglu(x, w_in, w_out):
    gate, up = jnp.split(x @ w_in, 2, axis=-1)
    return (jax.nn.silu(gate) * up) @ w_out


def t5_band_buckets():
    qi = np.arange(ATT_BLOCK)[:, None] + ATT_BLOCK
    kj = np.arange(2 * ATT_BLOCK)[None, :]
    dist = qi - kj
    n = np.maximum(dist, 0)
    max_exact = N_BUCKETS // 2
    large = max_exact + (np.log(np.maximum(n, 1) / max_exact)
                         / np.log(MAX_DISTANCE / max_exact)
                         * (N_BUCKETS - max_exact)).astype(np.int32)
    large = np.minimum(large, N_BUCKETS - 1)
    bucket = np.where(n < max_exact, n, large).astype(np.int32)
    return bucket, dist.astype(np.int32)


def sliding_window_attention(q, k, v, rel_table, sinks):
    B, S = q.shape[0], q.shape[1]
    nb = S // ATT_BLOCK
    G = N_Q_HEADS // N_KV_HEADS
    qb = q.reshape(B, nb, ATT_BLOCK, N_KV_HEADS, G, HEAD_DIM)

    def band(t):
        t = t.reshape(B, nb, ATT_BLOCK, N_KV_HEADS, HEAD_DIM)
        prev = jnp.pad(t, ((0, 0), (1, 0), (0, 0), (0, 0), (0, 0)))[:, :-1]
        return jnp.concatenate([prev, t], axis=2)

    kb, vb = band(k), band(v)
    s = jnp.einsum('bnqhgd,bnkhd->bnhgqk', qb, kb).astype(jnp.float32) * (HEAD_DIM ** -0.5)
    bucket, dist = t5_band_buckets()
    bias = rel_table.astype(jnp.float32)[bucket]
    bias = bias.transpose(2, 0, 1).reshape(N_KV_HEADS, G, ATT_BLOCK, 2 * ATT_BLOCK)
    key_pos = (jnp.arange(nb)[:, None, None] * ATT_BLOCK - ATT_BLOCK
               + jnp.arange(2 * ATT_BLOCK)[None, None, :])
    dist_j = jnp.asarray(dist)[None]
    valid = (dist_j >= 0) & (dist_j < WINDOW) & (key_pos >= 0)
    s = jnp.where(valid[None, :, None, None], s + bias, -jnp.inf)
    sink = sinks.astype(jnp.float32).reshape(N_KV_HEADS, G, 1, 1)
    m = jnp.maximum(jnp.max(s, axis=-1, keepdims=True), sink)
    e = jnp.exp(s - m)
    probs = e / (jnp.sum(e, axis=-1, keepdims=True) + jnp.exp(sink - m))
    o = jnp.einsum('bnhgqk,bnkhd->bnqhgd', probs.astype(v.dtype), vb)
    return o.reshape(B, S, ATT_Q_W)


def hgrn2_recurrence(q, f_logit, i, lb):
    B, S = q.shape[0], q.shape[1]
    nc = S // REC_CHUNK
    lbf = lb.astype(jnp.float32).reshape(REC_HEADS, REC_KEY_DIM)
    f = lbf + (1.0 - lbf) * jax.nn.sigmoid(f_logit.astype(jnp.float32))
    log_f = jnp.log(f)
    k = 1.0 - f

    def to_chunks(t):
        return t.reshape(B, nc, REC_CHUNK, REC_HEADS, t.shape[-1]).transpose(1, 0, 3, 2, 4)

    qc = to_chunks(q.astype(jnp.float32))
    kc = to_chunks(k)
    vc = to_chunks(i.astype(jnp.float32))
    gc = to_chunks(log_f)
    causal = jnp.tril(jnp.ones((REC_CHUNK, REC_CHUNK), dtype=bool))[:, :, None]

    def step(state, inp):
        qt, kt, vt, gt = inp
        b = jnp.cumsum(gt, axis=2)
        diff = b[:, :, :, None, :] - b[:, :, None, :, :]
        decay = jnp.exp(jnp.where(causal, diff, -jnp.inf))
        attn = jnp.einsum('bhtd,bhsd,bhtsd->bhts', qt, kt, decay)
        o = (jnp.einsum('bhts,bhsv->bhtv', attn, vt)
             + jnp.einsum('bhtd,bhdv->bhtv', qt * jnp.exp(b), state))
        b_last = b[:, :, -1:, :]
        new_state = (jnp.exp(b_last[:, :, 0, :])[..., None] * state
                     + jnp.einsum('bhsd,bhsv->bhdv', kt * jnp.exp(b_last - b), vt))
        return new_state, o

    s0 = jnp.zeros((B, REC_HEADS, REC_KEY_DIM, REC_VAL_DIM), jnp.float32)
    _, o = lax.scan(step, s0, (qc, kc, vc, gc))
    return o.transpose(1, 0, 3, 2, 4).reshape(B, S, REC_HEADS, REC_VAL_DIM).astype(q.dtype)


def setup_inputs(seed: int = 0) -> dict:
    key = jax.random.key(seed)
    ks = jax.random.split(key, 24)
    f32 = jnp.float32

    def nrm(k, shape, scale):
        return jax.random.normal(k, shape, f32) * scale

    def gain(k, shape):
        return 1.0 + 0.05 * jax.random.normal(k, shape, f32)

    return {
        "x": nrm(ks[0], (BATCH, SEQ, D_MODEL), 1.0),
        "p": nrm(ks[1], (DEPTH, BATCH, SEQ, PLE_DIM), 1.0),
        "rel_bias": nrm(ks[2], (N_BUCKETS, N_Q_HEADS), 0.5),
        "lb_param": nrm(ks[3], (DEPTH + 1, REC_K_W), 1.0),
        "norm_ffn1": gain(ks[4], (DEPTH, D_MODEL)),
        "w_ffn1_in": nrm(ks[5], (DEPTH, D_MODEL, 2 * D_FF), D_MODEL ** -0.5),
        "w_ffn1_out": nrm(ks[6], (DEPTH, D_FF, D_MODEL), D_FF ** -0.5),
        "norm_mix": gain(ks[7], (DEPTH, D_MODEL)),
        "w_in": nrm(ks[8], (DEPTH, D_MODEL, IN_W), D_MODEL ** -0.5),
        "attn_sinks": nrm(ks[9], (DEPTH, N_Q_HEADS), 1.0),
        "rec_norm": gain(ks[10], (DEPTH, REC_VAL_DIM)),
        "w_att_proj": nrm(ks[11], (DEPTH, ATT_Q_W, D_MODEL), ATT_Q_W ** -0.5),
        "w_rec_proj": nrm(ks[12], (DEPTH, REC_V_W, D_MODEL), REC_V_W ** -0.5),
        "w_out": nrm(ks[13], (DEPTH, D_MODEL, D_MODEL), D_MODEL ** -0.5),
        "norm_ffn2": gain(ks[14], (DEPTH, D_MODEL)),
        "w_ffn2_in": nrm(ks[15], (DEPTH, D_MODEL, 2 * D_FF), D_MODEL ** -0.5),
        "w_ffn2_out": nrm(ks[16], (DEPTH, D_FF, D_MODEL), D_FF ** -0.5),
        "norm_ple": gain(ks[17], (DEPTH, D_MODEL)),
        "w_ple_gate": nrm(ks[18], (DEPTH, D_MODEL, D_MODEL), D_MODEL ** -0.5),
        "w_ple_proj": nrm(ks[19], (DEPTH, PLE_DIM, D_MODEL), PLE_DIM ** -0.5),
        "norm_final": gain(ks[20], (D_MODEL,)),
    }


def reference(x, p, rel_bias, lb_param, norm_ffn1, w_ffn1_in, w_ffn1_out, norm_mix, w_in,
              attn_sinks, rec_norm, w_att_proj, w_rec_proj, w_out, norm_ffn2, w_ffn2_in,
              w_ffn2_out, norm_ple, w_ple_gate, w_ple_proj, norm_final):
    B, S = x.shape[0], x.shape[1]
    lower_bounds = jnp.cumsum(jax.nn.softmax(lb_param.astype(jnp.float32), axis=0), axis=0)
    splits = _split_points()
    h = x
    for layer in range(DEPTH):
        h = h + 0.5 * swiglu(rms_norm(h, norm_ffn1[layer]), w_ffn1_in[layer], w_ffn1_out[layer])

        u = rms_norm(h, norm_mix[layer])
        proj = u @ w_in[layer]
        aq, ak, av, rq, rf, ri, rg, ga, gb = jnp.split(proj, splits, axis=-1)

        att = sliding_window_attention(
            aq.reshape(B, S, N_Q_HEADS, HEAD_DIM),
            ak.reshape(B, S, N_KV_HEADS, HEAD_DIM),
            av.reshape(B, S, N_KV_HEADS, HEAD_DIM),
            rel_bias, attn_sinks[layer])

        rec = hgrn2_recurrence(
            rq.reshape(B, S, REC_HEADS, REC_KEY_DIM),
            rf.reshape(B, S, REC_HEADS, REC_KEY_DIM),
            ri.reshape(B, S, REC_HEADS, REC_VAL_DIM),
            lower_bounds[layer])
        rec = rms_norm(rec, rec_norm[layer]).reshape(B, S, REC_V_W) * jax.nn.sigmoid(rg)

        y_a = att @ w_att_proj[layer]
        y_b = rec @ w_rec_proj[layer]
        merged = jax.nn.sigmoid(ga) * y_a + jax.nn.sigmoid(gb) * y_b
        h = h + merged @ w_out[layer]

        h = h + 0.5 * swiglu(rms_norm(h, norm_ffn2[layer]), w_ffn2_in[layer], w_ffn2_out[layer])

        gate = jax.nn.sigmoid(rms_norm(h, norm_ple[layer]) @ w_ple_gate[layer])
        h = h + gate * (p[layer] @ w_ple_proj[layer])
    return rms_norm(h, norm_final)
```

```python
import functools

import numpy as np
import jax
import jax.numpy as jnp
from jax import lax
from jax.experimental import pallas as pl
from jax.experimental.pallas import tpu as pltpu

D_MODEL = 1024
DEPTH = 1
HEAD_DIM = 64
N_Q_HEADS = 8
N_KV_HEADS = 2
WINDOW = 128
ATT_BLOCK = 128
N_BUCKETS = 32
MAX_DISTANCE = 128
REC_HEADS = 4
REC_KEY_DIM = 128
REC_VAL_DIM = 128
D_FF = 2816
PLE_DIM = 256
EPS = 1e-6

ATT_Q_W = N_Q_HEADS * HEAD_DIM
ATT_KV_W = N_KV_HEADS * HEAD_DIM
REC_W = REC_HEADS * REC_KEY_DIM
IN_W = ATT_Q_W + 2 * ATT_KV_W + 4 * REC_W + 2 * D_MODEL

_O_AQ = 0
_O_AKV = _O_AQ + ATT_Q_W
_O_RQ = _O_AKV + 2 * ATT_KV_W
_O_RF = _O_RQ + REC_W
_O_RI = _O_RF + REC_W
_O_RG = _O_RI + REC_W
_O_GA = _O_RG + REC_W

LANES = 128
ROW_TILE = 256
FF_CHUNK = 256
REC_CHUNK = 128
REC_LEVELS = 7
VMEM_LIMIT = 56 * 1024 * 1024

F32 = jnp.float32
BF16 = jnp.bfloat16


def _dot(a, b):
    return jnp.dot(a, b, preferred_element_type=F32)


def _dot_nt(a, b):
    return lax.dot_general(a, b, (((1,), (1,)), ((), ())), preferred_element_type=F32)


def _dot_tn(a, b):
    return lax.dot_general(a, b, (((0,), (0,)), ((), ())), preferred_element_type=F32)


def _sigmoid(x):
    return 1.0 / (1.0 + jnp.exp(-x))


def _rms(x, g):
    ms = jnp.mean(x * x, axis=-1, keepdims=True)
    return x * lax.rsqrt(ms + EPS) * g


def _swiglu_half_step(h, g_ref, w_in_ref, w_out_ref, act_ref):
    u = _rms(h, g_ref[...]).astype(BF16)
    for j in range(D_FF // FF_CHUNK):
        gu = _dot(u, w_in_ref[:, 2 * FF_CHUNK * j:2 * FF_CHUNK * (j + 1)])
        gate, up = gu[:, :FF_CHUNK], gu[:, FF_CHUNK:]
        act_ref[:, FF_CHUNK * j:FF_CHUNK * (j + 1)] = (gate * _sigmoid(gate) * up).astype(BF16)
    return h + 0.5 * _dot(act_ref[...], w_out_ref[...])


def _pre_kernel(x_ref, g1_ref, w1i_ref, w1o_ref, g2_ref, win_ref,
                h1_ref, aq_ref, akv_ref, rq_ref, rf_ref, ri_ref, srg_ref, sgab_ref, act_ref):
    h1 = _swiglu_half_step(x_ref[...], g1_ref, w1i_ref, w1o_ref, act_ref)
    h1_ref[...] = h1
    u = _rms(h1, g2_ref[...]).astype(BF16)

    def proj(off, width):
        return _dot(u, win_ref[:, off:off + width])

    aq_ref[...] = proj(_O_AQ, ATT_Q_W).astype(BF16)
    akv_ref[...] = proj(_O_AKV, 2 * ATT_KV_W).astype(BF16)
    rq_ref[...] = proj(_O_RQ, REC_W).astype(BF16)
    rf_ref[...] = proj(_O_RF, REC_W)
    ri_ref[...] = proj(_O_RI, REC_W).astype(BF16)
    srg_ref[...] = _sigmoid(proj(_O_RG, REC_W)).astype(BF16)
    for j in range(2 * D_MODEL // 512):
        sgab_ref[:, 512 * j:512 * (j + 1)] = _sigmoid(proj(_O_GA + 512 * j, 512)).astype(BF16)


def _bias_kernel(rel_ref, bucket_ref, valid_ref, bias_ref):
    bucket = bucket_ref[...]
    valid = valid_ref[...] > 0
    for h in range(N_Q_HEADS):
        acc = jnp.zeros(bucket.shape, F32)
        for b in range(N_BUCKETS):
            acc = jnp.where(bucket == b, rel_ref[b, h], acc)
        bias_ref[h] = jnp.where(valid, acc, -jnp.inf)


def _att_kernel(q_ref, kvp_ref, kvc_ref, bias_ref, sink_ref, o_ref):
    n = pl.program_id(1)
    kvp, kvc = kvp_ref[0], kvc_ref[0]
    kslab = jnp.concatenate([kvp[:, :LANES], kvc[:, :LANES]], axis=0)
    vslab = jnp.concatenate([kvp[:, LANES:], kvc[:, LANES:]], axis=0)
    lane = lax.broadcasted_iota(jnp.int32, (ATT_BLOCK, LANES), 1)
    low = lane < HEAD_DIM
    col = lax.broadcasted_iota(jnp.int32, (ATT_BLOCK, 2 * ATT_BLOCK), 1)
    key_ok = (col >= ATT_BLOCK) | (n > 0)
    for sl in range(N_Q_HEADS // 2):
        qs = q_ref[0, :, LANES * sl:LANES * (sl + 1)]
        halves = []
        for half in range(2):
            h = sl + (N_Q_HEADS // 2) * half
            qm = jnp.where(low if half == 0 else ~low, qs, jnp.zeros_like(qs))
            s = _dot_nt(qm, kslab) * (HEAD_DIM ** -0.5) + bias_ref[h]
            s = jnp.where(key_ok, s, -jnp.inf)
            sink = sink_ref[h]
            m = jnp.maximum(jnp.max(s, axis=-1, keepdims=True), sink)
            e = jnp.exp(s - m)
            den = jnp.sum(e, axis=-1, keepdims=True) + jnp.exp(sink - m)
            halves.append(_dot(e.astype(BF16), vslab) / den)
        o_ref[0, :, LANES * sl:LANES * (sl + 1)] = jnp.where(low, halves[0], halves[1]).astype(BF16)


def _rec_tables():
    n = REC_CHUNK
    lc = np.tril(np.ones((n, n), np.float32))
    rows = np.arange(n)
    blocks = []
    for lvl in range(REC_LEVELS):
        half = 1 << lvl
        ref = (rows & ~(2 * half - 1)) + half - 1
        blocks.append(lc - lc[ref])
    blocks.append(lc)
    blocks.append(lc[n - 1][None, :] - lc)
    g = np.concatenate(blocks, axis=0)
    t, s = rows[:, None], rows[None, :]
    x = t ^ s
    lvl = np.where(x > 0, np.floor(np.log2(np.maximum(x, 1))).astype(np.int32), REC_LEVELS)
    lvl = np.where(s > t, -1, lvl).astype(np.int32)
    return np.concatenate([g, g], axis=1), lvl


def _rec_kernel(q_ref, f_ref, v_ref, sg_ref, lbp_ref, gn_ref, gmat_ref, lvl_ref, o_ref, st_ref, *, layer):
    n = REC_CHUNK

    @pl.when(pl.program_id(1) == 0)
    def _():
        st_ref[...] = jnp.zeros_like(st_ref)

    lbp = lbp_ref[...]
    ex = jnp.exp(lbp - jnp.max(lbp, axis=0, keepdims=True))
    lb = jnp.sum(ex[:layer + 1], axis=0, keepdims=True) / jnp.sum(ex, axis=0, keepdims=True)

    f = lb + (1.0 - lb) * _sigmoid(f_ref[0])
    g = jnp.log(f)
    k = 1.0 - f
    q = q_ref[0].astype(F32)
    g_hi = g.astype(BF16)
    g_lo = (g - g_hi.astype(F32)).astype(BF16)
    z = _dot(gmat_ref[...], jnp.concatenate([g_hi, g_lo], axis=0))

    row = lax.broadcasted_iota(jnp.int32, (n, REC_W), 0)
    lvl = lvl_ref[...]
    q_bf, k_bf = q.astype(BF16), k.astype(BF16)
    amat = []
    for hd in range(REC_HEADS):
        sl = slice(REC_KEY_DIM * hd, REC_KEY_DIM * (hd + 1))
        amat.append(jnp.where(lvl == REC_LEVELS, _dot_nt(q_bf[:, sl], k_bf[:, sl]), 0.0))
    for level in range(REC_LEVELS):
        w = jnp.exp(-jnp.abs(z[n * level:n * (level + 1)]))
        m = (jnp.where(((row >> level) & 1) == 1, q, k) * w).astype(BF16)
        for hd in range(REC_HEADS):
            sl = slice(REC_KEY_DIM * hd, REC_KEY_DIM * (hd + 1))
            amat[hd] = jnp.where(lvl == level, _dot_nt(m[:, sl], m[:, sl]), amat[hd])

    c = z[n * REC_LEVELS:n * (REC_LEVELS + 1)]
    suf = z[n * (REC_LEVELS + 1):n * (REC_LEVELS + 2)]
    qs = (q * jnp.exp(c)).astype(BF16)
    ks = (k * jnp.exp(suf)).astype(BF16)
    e_last = jnp.exp(c[n - 1:n, :])
    v = v_ref[0]
    for hd in range(REC_HEADS):
        sl = slice(REC_KEY_DIM * hd, REC_KEY_DIM * (hd + 1))
        st = st_ref[:, sl]
        o = _dot(amat[hd].astype(BF16), v[:, sl]) + _dot_nt(qs[:, sl], st.astype(BF16))
        st_ref[:, sl] = st * e_last[:, sl] + _dot_tn(v[:, sl], ks[:, sl])
        y = _rms(o, gn_ref[...]) * sg_ref[0, :, sl].astype(F32)
        o_ref[0, :, sl] = y.astype(BF16)


def _post_kernel(att_ref, rec_ref, sgab_ref, h1_ref, p_ref, wa_ref, wr_ref, wo_ref,
                 g2_ref, w2i_ref, w2o_ref, gp_ref, wpg_ref, wpp_ref, gf_ref, out_ref, act_ref):
    ya = _dot(att_ref[...], wa_ref[...])
    yb = _dot(rec_ref[...], wr_ref[...])
    merged = sgab_ref[:, :D_MODEL].astype(F32) * ya + sgab_ref[:, D_MODEL:].astype(F32) * yb
    h2 = h1_ref[...] + _dot(merged.astype(BF16), wo_ref[...])
    h3 = _swiglu_half_step(h2, g2_ref, w2i_ref, w2o_ref, act_ref)
    gate = _sigmoid(_dot(_rms(h3, gp_ref[...]).astype(BF16), wpg_ref[...]))
    h4 = h3 + gate * _dot(p_ref[...].astype(BF16), wpp_ref[...])
    out_ref[...] = _rms(h4, gf_ref[...])


def _band_tables():
    qi = np.arange(ATT_BLOCK)[:, None] + ATT_BLOCK
    kj = np.arange(2 * ATT_BLOCK)[None, :]
    dist = qi - kj
    n = np.maximum(dist, 0)
    max_exact = N_BUCKETS // 2
    large = max_exact + (np.log(np.maximum(n, 1) / max_exact)
                         / np.log(MAX_DISTANCE / max_exact)
                         * (N_BUCKETS - max_exact)).astype(np.int32)
    large = np.minimum(large, N_BUCKETS - 1)
    bucket = np.where(n < max_exact, n, large).astype(np.int32)
    valid = ((dist >= 0) & (dist < WINDOW)).astype(np.int32)
    return bucket, valid


def _resident(shape):
    return pl.BlockSpec(shape, lambda *_: (0,) * len(shape), pipeline_mode=pl.Buffered(1))


def _rows(width, tile=ROW_TILE):
    return pl.BlockSpec((tile, width), lambda i: (i, 0))


def _swiglu_in_layout(w):
    nj = D_FF // FF_CHUNK
    w = w.reshape(D_MODEL, 2, nj, FF_CHUNK).transpose(0, 2, 1, 3)
    return w.reshape(D_MODEL, 2 * D_FF).astype(BF16)


def _params(semantics):
    return pltpu.CompilerParams(dimension_semantics=semantics, vmem_limit_bytes=VMEM_LIMIT)


def kernel(x, p, rel_bias, lb_param, norm_ffn1, w_ffn1_in, w_ffn1_out, norm_mix, w_in, attn_sinks, rec_norm, w_att_proj, w_rec_proj, w_out, norm_ffn2, w_ffn2_in, w_ffn2_out, norm_ple, w_ple_gate, w_ple_proj, norm_final):
    assert DEPTH == 1 and x.shape[-1] == D_MODEL and x.shape[1] % REC_CHUNK == 0
    layer = 0
    B, S = x.shape[0], x.shape[1]
    T = B * S
    assert T % ROW_TILE == 0
    n_tiles = T // ROW_TILE

    head_perm = np.arange(N_Q_HEADS).reshape(2, N_Q_HEADS // 2).T.reshape(-1)
    qcols = (head_perm[:, None] * HEAD_DIM + np.arange(HEAD_DIM)[None, :]).reshape(-1)
    w_in_l = w_in[layer]
    w_in_bf = jnp.concatenate([w_in_l[:, qcols], w_in_l[:, ATT_Q_W:]], axis=1).astype(BF16)
    w_att_bf = w_att_proj[layer][qcols, :].astype(BF16)

    row2 = lambda a: a.reshape(1, -1).astype(F32)

    pre_out = pl.pallas_call(
        _pre_kernel,
        grid=(n_tiles,),
        in_specs=[_rows(D_MODEL), _resident((1, D_MODEL)), _resident((D_MODEL, 2 * D_FF)),
                  _resident((D_FF, D_MODEL)), _resident((1, D_MODEL)), _resident((D_MODEL, IN_W))],
        out_specs=[_rows(D_MODEL), _rows(ATT_Q_W), _rows(2 * ATT_KV_W), _rows(REC_W), _rows(REC_W),
                   _rows(REC_W), _rows(REC_W), _rows(2 * D_MODEL)],
        out_shape=[jax.ShapeDtypeStruct((T, D_MODEL), F32),
                   jax.ShapeDtypeStruct((T, ATT_Q_W), BF16),
                   jax.ShapeDtypeStruct((T, 2 * ATT_KV_W), BF16),
                   jax.ShapeDtypeStruct((T, REC_W), BF16),
                   jax.ShapeDtypeStruct((T, REC_W), F32),
                   jax.ShapeDtypeStruct((T, REC_W), BF16),
                   jax.ShapeDtypeStruct((T, REC_W), BF16),
                   jax.ShapeDtypeStruct((T, 2 * D_MODEL), BF16)],
        scratch_shapes=[pltpu.VMEM((ROW_TILE, D_FF), BF16)],
        compiler_params=_params(("parallel",)),
        name="pre",
    )(x.reshape(T, D_MODEL), row2(norm_ffn1[layer]), _swiglu_in_layout(w_ffn1_in[layer]),
      w_ffn1_out[layer].astype(BF16), row2(norm_mix[layer]), w_in_bf)
    h1, aq, akv, rq, rf, ri, srg, sgab = pre_out

    bucket, valid = _band_tables()
    bias = pl.pallas_call(
        _bias_kernel,
        in_specs=[pl.BlockSpec(memory_space=pltpu.SMEM), pl.BlockSpec(memory_space=pltpu.VMEM),
                  pl.BlockSpec(memory_space=pltpu.VMEM)],
        out_specs=pl.BlockSpec(memory_space=pltpu.VMEM),
        out_shape=jax.ShapeDtypeStruct((N_Q_HEADS, ATT_BLOCK, 2 * ATT_BLOCK), F32),
        name="t5_bias",
    )(rel_bias.astype(F32), jnp.asarray(bucket), jnp.asarray(valid))

    nb = S // ATT_BLOCK
    att = pl.pallas_call(
        _att_kernel,
        grid=(B, nb),
        in_specs=[pl.BlockSpec((1, ATT_BLOCK, ATT_Q_W), lambda b, n: (b, n, 0)),
                  pl.BlockSpec((1, ATT_BLOCK, 2 * ATT_KV_W), lambda b, n: (b, jnp.maximum(n - 1, 0), 0)),
                  pl.BlockSpec((1, ATT_BLOCK, 2 * ATT_KV_W), lambda b, n: (b, n, 0)),
                  _resident((N_Q_HEADS, ATT_BLOCK, 2 * ATT_BLOCK)),
                  pl.BlockSpec(memory_space=pltpu.SMEM)],
        out_specs=pl.BlockSpec((1, ATT_BLOCK, ATT_Q_W), lambda b, n: (b, n, 0)),
        out_shape=jax.ShapeDtypeStruct((B, S, ATT_Q_W), BF16),
        compiler_params=_params(("parallel", "parallel")),
        name="swa",
    )(aq.reshape(B, S, ATT_Q_W), akv.reshape(B, S, 2 * ATT_KV_W), akv.reshape(B, S, 2 * ATT_KV_W),
      bias, attn_sinks[layer].astype(F32))

    gmat, lvl = _rec_tables()
    seq = lambda w: pl.BlockSpec((1, REC_CHUNK, w), lambda b, c: (b, c, 0))
    rec = pl.pallas_call(
        functools.partial(_rec_kernel, layer=layer),
        grid=(B, S // REC_CHUNK),
        in_specs=[seq(REC_W), seq(REC_W), seq(REC_W), seq(REC_W),
                  _resident((DEPTH + 1, REC_W)), _resident((1, REC_VAL_DIM)),
                  _resident(gmat.shape), _resident(lvl.shape)],
        out_specs=seq(REC_W),
        out_shape=jax.ShapeDtypeStruct((B, S, REC_W), BF16),
        scratch_shapes=[pltpu.VMEM((REC_VAL_DIM, REC_W), F32)],
        compiler_params=_params(("parallel", "arbitrary")),
        name="hgrn2",
    )(rq.reshape(B, S, REC_W), rf.reshape(B, S, REC_W), ri.reshape(B, S, REC_W), srg.reshape(B, S, REC_W),
      lb_param.astype(F32), row2(rec_norm[layer]), jnp.asarray(gmat, BF16), jnp.asarray(lvl))

    out = pl.pallas_call(
        _post_kernel,
        grid=(n_tiles,),
        in_specs=[_rows(ATT_Q_W), _rows(REC_W), _rows(2 * D_MODEL), _rows(D_MODEL), _rows(PLE_DIM),
                  _resident((ATT_Q_W, D_MODEL)), _resident((REC_W, D_MODEL)), _resident((D_MODEL, D_MODEL)),
                  _resident((1, D_MODEL)), _resident((D_MODEL, 2 * D_FF)), _resident((D_FF, D_MODEL)),
                  _resident((1, D_MODEL)), _resident((D_MODEL, D_MODEL)), _resident((PLE_DIM, D_MODEL)),
                  _resident((1, D_MODEL))],
        out_specs=_rows(D_MODEL),
        out_shape=jax.ShapeDtypeStruct((T, D_MODEL), F32),
        scratch_shapes=[pltpu.VMEM((ROW_TILE, D_FF), BF16)],
        compiler_params=_params(("parallel",)),
        name="post",
    )(att.reshape(T, ATT_Q_W), rec.reshape(T, REC_W), sgab, h1, p[layer].reshape(T, PLE_DIM),
      w_att_bf, w_rec_proj[layer].astype(BF16), w_out[layer].astype(BF16),
      row2(norm_ffn2[layer]), _swiglu_in_layout(w_ffn2_in[layer]), w_ffn2_out[layer].astype(BF16),
      row2(norm_ple[layer]), w_ple_gate[layer].astype(BF16), w_ple_proj[layer].astype(BF16),
      row2(norm_final))
    return out.reshape(B, S, D_MODEL)
```
